```python
import math
import jax, jax.numpy as jnp
from jax import lax
import numpy as np

D_MODEL = 1024
BATCH = 16
SEQ = 2048
DEPTH = 1

D_MIX = D_MODEL
DIFF_WIDTH = D_MIX // 2
GLA_WIDTH = D_MIX - DIFF_WIDTH
DIFF_HEADS = 4
DIFF_V_DIM = DIFF_WIDTH // DIFF_HEADS
DIFF_QK_DIM = DIFF_V_DIM // 2
ROT_DIM = DIFF_QK_DIM // 4
ROPE_THETA = 500000.0
Q_BLOCK = 128
GLA_HEADS = 4
GLA_V_DIM = GLA_WIDTH // GLA_HEADS
GLA_K_DIM = GLA_V_DIM // 2
GK_RANK = 16
GATE_NORMALIZER = 16.0
CHUNK = 64
D_DQ = DIFF_HEADS * 2 * DIFF_QK_DIM
D_DK = DIFF_HEADS * 2 * DIFF_QK_DIM
D_DV = DIFF_WIDTH
D_DG = DIFF_WIDTH
D_GQ = GLA_HEADS * GLA_K_DIM
D_GK = GLA_HEADS * GLA_K_DIM
D_GV = GLA_WIDTH
D_GG = GLA_WIDTH
D_IN = D_DQ + D_DK + D_DV + D_DG + D_GQ + D_GK + D_GV + D_GG + GK_RANK
NORM_EPS = 1e-6
NEG_INF = -1e30

kernel_name = "hybrid_diffattn_gla_parallel_heads"


def rms_norm(x, gain, eps=NORM_EPS):
    xf = x.astype(jnp.float32)
    y = xf * lax.rsqrt(jnp.mean(xf * xf, axis=-1, keepdims=True) + eps)
    return (y * gain.astype(jnp.float32)).astype(x.dtype)


def partial_rope(x, cos, sin):
    half = ROT_DIM // 2
    x1 = x[..., :half]
    x2 = x[..., half:ROT_DIM]
    return jnp.concatenate([x1 * cos - x2 * sin, x2 * cos + x1 * sin, x[..., ROT_DIM:]], axis=-1)


def split_columns(proj):
    sizes = [D_DQ, D_DK, D_DV, D_DG, D_GQ, D_GK, D_GV, D_GG, GK_RANK]
    idx = [int(v) for v in np.cumsum(sizes)[:-1]]
    return jnp.split(proj, idx, axis=-1)


def diff_attention(q, k, v, lam):
    b, s, h, _, d = q.shape
    qs = q.transpose(0, 2, 3, 1, 4)
    ks = k.transpose(0, 2, 3, 1, 4)
    vs = v.transpose(0, 2, 1, 3)
    scale = 1.0 / math.sqrt(d)
    kpos = jnp.arange(s)
    n_blocks = s // Q_BLOCK

    def block(i):
        qb = lax.dynamic_slice_in_dim(qs, i * Q_BLOCK, Q_BLOCK, axis=3)
        sc = jnp.einsum('bhcqd,bhckd->bhcqk', qb, ks).astype(jnp.float32) * scale
        qpos = i * Q_BLOCK + jnp.arange(Q_BLOCK)
        sc = jnp.where(kpos[None, :] <= qpos[:, None], sc, NEG_INF)
        p = jax.nn.softmax(sc, axis=-1)
        a = p[:, :, 0] - lam * p[:, :, 1]
        return jnp.einsum('bhqk,bhkv->bhqv', a.astype(vs.dtype), vs)

    o = lax.map(block, jnp.arange(n_blocks))
    return o.transpose(1, 0, 3, 2, 4).reshape(b, s, h, vs.shape[-1])


def gla_chunked(q, k, v, g):
    b, s, h, dk = q.shape
    dv = v.shape[-1]
    nc = s // CHUNK

    def to_chunks(t):
        return t.astype(jnp.float32).reshape(b, nc, CHUNK, h, t.shape[-1]).transpose(1, 0, 3, 2, 4)

    qc, kc, vc, gc = to_chunks(q), to_chunks(k), to_chunks(v), to_chunks(g)
    tril = jnp.tril(jnp.ones((CHUNK, CHUNK), dtype=bool))

    def step(state, inp):
        qi, ki, vi, gi = inp
        bcum = jnp.cumsum(gi, axis=2)
        o_inter = jnp.einsum('bhcd,bhde->bhce', qi * jnp.exp(bcum), state)
        diff = bcum[:, :, :, None, :] - bcum[:, :, None, :, :]
        m = tril[None, None, :, :, None]
        decay = jnp.where(m, jnp.exp(jnp.where(m, diff, 0.0)), 0.0)
        attn = jnp.einsum('bhid,bhjd,bhijd->bhij', qi, ki, decay)
        o_intra = jnp.einsum('bhij,bhje->bhie', attn, vi)
        b_last = bcum[:, :, -1, :]
        k_dec = ki * jnp.exp(b_last[:, :, None, :] - bcum)
        new_state = jnp.exp(b_last)[..., None] * state + jnp.einsum('bhcd,bhce->bhde', k_dec, vi)
        return new_state, o_inter + o_intra

    s0 = jnp.zeros((b, h, dk, dv), jnp.float32)
    _, o = lax.scan(step, s0, (qc, kc, vc, gc))
    return o.transpose(1, 0, 3, 2, 4).reshape(b, s, h, dv).astype(v.dtype)


def setup_inputs(seed: int = 0) -> dict:
    key = jax.random.key(seed)
    ks = jax.random.split(key, 16)
    f = jnp.float32
    x = jax.random.normal(ks[0], (BATCH, SEQ, D_MODEL), f)
    norm_gain = 1.0 + 0.02 * jax.random.normal(ks[1], (DEPTH, D_MODEL), f)
    w_in = jax.random.normal(ks[2], (DEPTH, D_MODEL, D_IN), f) * D_MODEL ** -0.5
    q_norm_gain = 1.0 + 0.02 * jax.random.normal(ks[3], (DEPTH, DIFF_QK_DIM), f)
    k_norm_gain = 1.0 + 0.02 * jax.random.normal(ks[4], (DEPTH, DIFF_QK_DIM), f)
    lambda_q1 = 0.1 * jax.random.normal(ks[5], (DEPTH, DIFF_QK_DIM), f)
    lambda_k1 = 0.1 * jax.random.normal(ks[6], (DEPTH, DIFF_QK_DIM), f)
    lambda_q2 = 0.1 * jax.random.normal(ks[7], (DEPTH, DIFF_QK_DIM), f)
    lambda_k2 = 0.1 * jax.random.normal(ks[8], (DEPTH, DIFF_QK_DIM), f)
    diff_out_gain = 1.0 + 0.02 * jax.random.normal(ks[9], (DEPTH, DIFF_V_DIM), f)
    gk_up = jax.random.normal(ks[10], (DEPTH, GK_RANK, D_GK), f) * GK_RANK ** -0.5
    gk_bias = 0.01 * jax.random.normal(ks[11], (DEPTH, D_GK), f)
    gla_out_gain = 1.0 + 0.02 * jax.random.normal(ks[12], (DEPTH, GLA_V_DIM), f)
    w_out = jax.random.normal(ks[13], (DEPTH, D_MIX, D_MODEL), f) * D_MIX ** -0.5
    return {"x": x, "norm_gain": norm_gain, "w_in": w_in,
            "q_norm_gain": q_norm_gain, "k_norm_gain": k_norm_gain,
            "lambda_q1": lambda_q1, "lambda_k1": lambda_k1,
            "lambda_q2": lambda_q2, "lambda_k2": lambda_k2,
            "diff_out_gain": diff_out_gain, "gk_up": gk_up, "gk_bias": gk_bias,
            "gla_out_gain": gla_out_gain, "w_out": w_out}


def reference(x, norm_gain, w_in, q_norm_gain, k_norm_gain, lambda_q1, lambda_k1,
              lambda_q2, lambda_k2, diff_out_gain, gk_up, gk_bias, gla_out_gain, w_out):
    b, s, _ = x.shape
    pos = jnp.arange(s, dtype=jnp.float32)
    inv_freq = ROPE_THETA ** (-jnp.arange(0, ROT_DIM, 2, dtype=jnp.float32) / ROT_DIM)
    ang = pos[:, None] * inv_freq[None, :]
    cos = jnp.cos(ang)[:, None, None, :].astype(x.dtype)
    sin = jnp.sin(ang)[:, None, None, :].astype(x.dtype)

    for l in range(DEPTH):
        lambda_init = 0.8 - 0.6 * math.exp(-0.3 * l)
        h = rms_norm(x, norm_gain[l])
        proj = jnp.einsum('bsd,de->bse', h, w_in[l])
        dq, dk, dv, dg, gq, gk, gv, gg, gk_low = split_columns(proj)

        dq = dq.reshape(b, s, DIFF_HEADS, 2, DIFF_QK_DIM)
        dk = dk.reshape(b, s, DIFF_HEADS, 2, DIFF_QK_DIM)
        dv = dv.reshape(b, s, DIFF_HEADS, DIFF_V_DIM)
        dq = partial_rope(rms_norm(dq, q_norm_gain[l]), cos, sin)
        dk = partial_rope(rms_norm(dk, k_norm_gain[l]), cos, sin)
        lam = (jnp.exp(jnp.sum(lambda_q1[l].astype(jnp.float32) * lambda_k1[l].astype(jnp.float32)))
               - jnp.exp(jnp.sum(lambda_q2[l].astype(jnp.float32) * lambda_k2[l].astype(jnp.float32)))
               + lambda_init)
        o_a = diff_attention(dq, dk, dv, lam)
        o_a = rms_norm(o_a, diff_out_gain[l]) * (1.0 - lambda_init)
        o_a = o_a.reshape(b, s, DIFF_WIDTH) * jax.nn.silu(dg)

        gq = gq.reshape(b, s, GLA_HEADS, GLA_K_DIM) * (GLA_K_DIM ** -0.5)
        gk = gk.reshape(b, s, GLA_HEADS, GLA_K_DIM)
        gv = gv.reshape(b, s, GLA_HEADS, GLA_V_DIM)
        g_logit = jnp.einsum('bsr,re->bse', gk_low, gk_up[l]) + gk_bias[l]
        g_log = (jax.nn.log_sigmoid(g_logit.astype(jnp.float32)) / GATE_NORMALIZER)
        g_log = g_log.reshape(b, s, GLA_HEADS, GLA_K_DIM)
        o_b = gla_chunked(gq, gk, gv, g_log)
        o_b = rms_norm(o_b, gla_out_gain[l]).reshape(b, s, GLA_WIDTH) * jax.nn.silu(gg)

        o = jnp.concatenate([o_a, o_b], axis=-1)
        x = x + jnp.einsum('bse,ed->bsd', o, w_out[l])
    return x
```

```python
import functools
import math

import jax
import jax.numpy as jnp
from jax import lax
from jax.experimental import pallas as pl
from jax.experimental.pallas import tpu as pltpu

D_MODEL = 1024
DIFF_HEADS = 4
DIFF_V_DIM = 128
DIFF_QK_DIM = 64
ROT_DIM = 16
ROPE_THETA = 500000.0
GLA_HEADS = 4
GLA_V_DIM = 128
GLA_K_DIM = 64
GK_RANK = 16
GATE_NORMALIZER = 16.0
NORM_EPS = 1e-6
NEG_INF = -1e30
LAMBDA_INIT = 0.8 - 0.6 * math.exp(-0.3 * 0)

W_DIFF = DIFF_HEADS * DIFF_V_DIM
W_GQK = GLA_HEADS * GLA_K_DIM
W_GV = GLA_HEADS * GLA_V_DIM

LANES = 128
VMEM_LIMIT = 56 * 1024 * 1024

ROW_TILE = 512
Q_TILE = 256
KV_TILE = 256
GLA_CHUNK = 128

BF16 = jnp.bfloat16
F32 = jnp.float32


def _dot(a, b):
    return jnp.dot(a, b, preferred_element_type=F32)


def _dot_nt(a, b):
    return lax.dot_general(a, b, (((1,), (1,)), ((), ())), preferred_element_type=F32)


def _dot_tn(a, b):
    return lax.dot_general(a, b, (((0,), (0,)), ((), ())), preferred_element_type=F32)


def _split3(x):
    hi = x.astype(BF16)
    r = x - hi.astype(F32)
    mid = r.astype(BF16)
    lo = (r - mid.astype(F32)).astype(BF16)
    return hi, mid, lo


def _group_norm_rope(y, gain, cos_t, sin_t):
    lane = lax.broadcasted_iota(jnp.int32, y.shape, 1)
    low = lane < DIFF_QK_DIM
    sq = y * y
    s_lo = jnp.sum(jnp.where(low, sq, 0.0), axis=-1, keepdims=True)
    s_hi = jnp.sum(jnp.where(low, 0.0, sq), axis=-1, keepdims=True)
    ms = jnp.where(low, s_lo, s_hi) * (1.0 / DIFF_QK_DIM)
    y = y * lax.rsqrt(ms + NORM_EPS) * gain
    half = ROT_DIM // 2
    first = (lane % DIFF_QK_DIM) < half
    partner = jnp.where(first, pltpu.roll(y, LANES - half, 1), pltpu.roll(y, half, 1))
    return y * cos_t + partner * sin_t


def _in_proj_kernel(x_ref, ng_ref, cos_ref, sin_ref, wq_ref, wk_ref, wv_ref, wdg_ref, wgq_ref, wgk_ref,
                    wgv_ref, wgg_ref, wlow_ref, gkup_ref, gkb_ref, qng_ref, kng_ref,
                    q_out, k_out, v_out, dg_out, gq_out, gk_out, gv_out, gg_out, glog_out):
    x = x_ref[...]
    ms = jnp.mean(x * x, axis=-1, keepdims=True)
    h = (x * lax.rsqrt(ms + NORM_EPS) * ng_ref[...]).astype(BF16)

    cos_t = cos_ref[...]
    sin_t = sin_ref[...]
    q = _dot(h, wq_ref[...])
    k = _dot(h, wk_ref[...])
    for c in range(W_DIFF // LANES):
        sl = slice(c * LANES, (c + 1) * LANES)
        qc = _group_norm_rope(q[:, sl], qng_ref[...], cos_t, sin_t) * (1.0 / math.sqrt(DIFF_QK_DIM))
        q_out[:, sl] = qc.astype(BF16)
        k_out[:, sl] = _group_norm_rope(k[:, sl], kng_ref[...], cos_t, sin_t).astype(BF16)

    v_out[...] = _dot(h, wv_ref[...]).astype(BF16)
    dg_out[...] = _dot(h, wdg_ref[...]).astype(BF16)
    gq_out[...] = _dot(h, wgq_ref[...]) * (GLA_K_DIM ** -0.5)
    gk_out[...] = _dot(h, wgk_ref[...])
    gv_out[...] = _dot(h, wgv_ref[...]).astype(BF16)
    gg_out[...] = _dot(h, wgg_ref[...]).astype(BF16)

    low = _dot(h, wlow_ref[...])
    logit = _dot(low, gkup_ref[...]) + gkb_ref[...]
    glog_out[...] = jax.nn.log_sigmoid(logit) * (1.0 / GATE_NORMALIZER)


def _in_proj(x2, norm_gain, cos_t, sin_t, weights, gk_up, gk_bias, qng, kng, seq):
    n = x2.shape[0]
    tm = ROW_TILE
    tiles_per_seq = seq // tm
    row = lambda i: (i, 0)
    full = lambda i: (0, 0)
    tab = lambda i: (i % tiles_per_seq, 0)
    in_specs = [pl.BlockSpec((tm, D_MODEL), row), pl.BlockSpec((1, D_MODEL), full),
                pl.BlockSpec((tm, LANES), tab), pl.BlockSpec((tm, LANES), tab)]
    in_specs += [pl.BlockSpec(w.shape, full) for w in weights]
    in_specs += [pl.BlockSpec(gk_up.shape, full), pl.BlockSpec(gk_bias.shape, full),
                 pl.BlockSpec(qng.shape, full), pl.BlockSpec(kng.shape, full)]
    widths_dtypes = [(W_DIFF, BF16), (W_DIFF, BF16), (W_DIFF, BF16), (W_DIFF, BF16),
                     (W_GQK, F32), (W_GQK, F32), (W_GV, BF16), (W_GV, BF16), (W_GQK, F32)]
    out_shape = [jax.ShapeDtypeStruct((n, w), dt) for w, dt in widths_dtypes]
    out_specs = [pl.BlockSpec((tm, w), row) for w, _ in widths_dtypes]
    return pl.pallas_call(
        _in_proj_kernel,
        grid=(n // tm,),
        in_specs=in_specs,
        out_specs=out_specs,
        out_shape=out_shape,
        compiler_params=pltpu.CompilerParams(dimension_semantics=("arbitrary",), vmem_limit_bytes=VMEM_LIMIT),
        name="in_proj",
    )(x2, norm_gain, cos_t, sin_t, *weights, gk_up, gk_bias, qng, kng)


def _diff_attn_kernel(lam_ref, q_ref, k_ref, v_ref, og_ref, o_ref, qs_ref, m_ref, acc_ref):
    i = pl.program_id(2)
    tq, tk = Q_TILE, KV_TILE

    q = q_ref[...]
    lane = lax.broadcasted_iota(jnp.int32, q.shape, 1)
    zero = jnp.zeros_like(q)
    qs_ref[0:tq, :] = jnp.where(lane < DIFF_QK_DIM, q, zero)
    qs_ref[tq:2 * tq, :] = jnp.where(lane < DIFF_QK_DIM, zero, q)
    m_ref[...] = jnp.full(m_ref.shape, NEG_INF, F32)
    acc_ref[...] = jnp.zeros(acc_ref.shape, F32)
    ones = jnp.ones((tk, LANES), BF16)

    def step(j, masked):
        start = pl.multiple_of(j * tk, tk)
        k = k_ref[pl.ds(start, tk), :]
        v = v_ref[pl.ds(start, tk), :]
        s = _dot_nt(qs_ref[...], k)
        if masked:
            r = lax.broadcasted_iota(jnp.int32, s.shape, 0) % tq
            c = lax.broadcasted_iota(jnp.int32, s.shape, 1)
            s = jnp.where(c <= r, s, NEG_INF)
        m_old = m_ref[...]
        m_new = jnp.maximum(m_old, jnp.max(s, axis=-1, keepdims=True))
        alpha = jnp.exp(m_old - m_new)
        p = jnp.exp(s - m_new[:, 0:1]).astype(BF16)
        pv = _dot(p, jnp.concatenate([v, ones], axis=1))
        acc_ref[...] = jnp.concatenate([alpha, alpha], axis=1) * acc_ref[...] + pv
        m_ref[...] = m_new

    def body(j, carry):
        step(j, masked=False)
        return carry

    lax.fori_loop(0, i, body, 0)
    step(i, masked=True)

    lp = lam_ref[...]
    l1 = jnp.sum(lp[0:1] * lp[1:2], axis=-1, keepdims=True)
    l2 = jnp.sum(lp[2:3] * lp[3:4], axis=-1, keepdims=True)
    lam = jnp.exp(l1) - jnp.exp(l2) + LAMBDA_INIT
    acc = acc_ref[...]
    o1 = acc[0:tq, 0:LANES] / acc[0:tq, LANES:]
    o2 = acc[tq:, 0:LANES] / acc[tq:, LANES:]
    o = o1 - lam * o2
    ms = jnp.mean(o * o, axis=-1, keepdims=True)
    o = o * lax.rsqrt(ms + NORM_EPS) * og_ref[...] * (1.0 - LAMBDA_INIT)
    o_ref[...] = o.astype(BF16)


def _diff_attn(lam_params, q, k, v, out_gain, batch, seq):
    n = q.shape[0]
    tq = Q_TILE
    nq = seq // tq
    qmap = lambda b, h, i: (b * nq + i, h)
    kvmap = lambda b, h, i: (b, h)
    full = lambda b, h, i: (0, 0)
    return pl.pallas_call(
        _diff_attn_kernel,
        grid=(batch, DIFF_HEADS, nq),
        in_specs=[pl.BlockSpec(lam_params.shape, full),
                  pl.BlockSpec((tq, LANES), qmap),
                  pl.BlockSpec((seq, LANES), kvmap),
                  pl.BlockSpec((seq, LANES), kvmap),
                  pl.BlockSpec(out_gain.shape, full)],
        out_specs=pl.BlockSpec((tq, LANES), qmap),
        out_shape=jax.ShapeDtypeStruct((n, W_DIFF), BF16),
        scratch_shapes=[pltpu.VMEM((2 * tq, LANES), BF16),
                        pltpu.VMEM((2 * tq, LANES), F32),
                        pltpu.VMEM((2 * tq, 2 * LANES), F32)],
        compiler_params=pltpu.CompilerParams(dimension_semantics=("arbitrary",) * 3, vmem_limit_bytes=VMEM_LIMIT),
        name="diff_attn",
    )(lam_params, q, k, v, out_gain)


def _gla_kernel(q_ref, k_ref, g_ref, v_ref, og_ref, o_ref, qe_ref, ds_ref, dec_ref, oi_ref, *, seq):
    c_rows = GLA_CHUNK
    n_chunks = seq // c_rows
    ri = lax.broadcasted_iota(jnp.int32, (c_rows, c_rows), 0)
    ci = lax.broadcasted_iota(jnp.int32, (c_rows, c_rows), 1)
    causal = ci <= ri
    tri = jnp.where(causal, 1.0, 0.0).astype(BF16)
    head_of_klane = lax.broadcasted_iota(jnp.int32, (c_rows, W_GQK), 1) // GLA_K_DIM

    def chunk(c, carry):
        rows = pl.ds(pl.multiple_of(c * c_rows, c_rows), c_rows)
        g = g_ref[rows, :]
        g_hi, g_mid, g_lo = _split3(g)
        b = _dot(tri, g_hi) + _dot(tri, g_mid) + _dot(tri, g_lo)
        b_last = b[c_rows - 1:c_rows, :]
        b_mid = b[c_rows // 2 - 1:c_rows // 2, :]
        q = q_ref[rows, :]
        k = k_ref[rows, :]
        v = v_ref[rows, :]
        qt = (q * jnp.exp(b - b_mid)).astype(BF16)
        kt = (k * jnp.exp(b_mid - b)).astype(BF16)
        zero = jnp.zeros_like(qt)
        for h in range(GLA_HEADS):
            qh = jnp.where(head_of_klane == h, qt, zero)
            a = _dot_nt(qh, kt)
            a = jnp.where(causal, a, 0.0).astype(BF16)
            oi_ref[rows, h * GLA_V_DIM:(h + 1) * GLA_V_DIM] = _dot(a, v[:, h * GLA_V_DIM:(h + 1) * GLA_V_DIM])
        qe_ref[rows, :] = (q * jnp.exp(b)).astype(BF16)
        kd = (k * jnp.exp(b_last - b)).astype(BF16)
        ds_ref[c] = _dot_tn(v, kd)
        dec_ref[c] = jnp.exp(b_last)
        return carry

    lax.fori_loop(0, n_chunks, chunk, 0)

    head_of_vrow = lax.broadcasted_iota(jnp.int32, (W_GV, W_GQK), 0) // GLA_V_DIM
    head_of_kcol = lax.broadcasted_iota(jnp.int32, (W_GV, W_GQK), 1) // GLA_K_DIM
    same_head = head_of_vrow == head_of_kcol
    gain = og_ref[...]

    def emit(c, state_t):
        rows = pl.ds(pl.multiple_of(c * c_rows, c_rows), c_rows)
        st = jnp.where(same_head, state_t, 0.0).astype(BF16)
        o = oi_ref[rows, :] + _dot_nt(qe_ref[rows, :], st)
        for h in range(GLA_HEADS):
            sl = slice(h * GLA_V_DIM, (h + 1) * GLA_V_DIM)
            oh = o[:, sl]
            ms = jnp.mean(oh * oh, axis=-1, keepdims=True)
            o_ref[rows, sl] = (oh * lax.rsqrt(ms + NORM_EPS) * gain).astype(BF16)
        return dec_ref[c] * state_t + ds_ref[c]

    lax.fori_loop(0, n_chunks, emit, jnp.zeros((W_GV, W_GQK), F32))


def _gla(gq, gk, glog, gv, out_gain, batch, seq):
    n = gq.shape[0]
    n_chunks = seq // GLA_CHUNK
    bmap = lambda b: (b, 0)
    full = lambda b: (0, 0)
    return pl.pallas_call(
        functools.partial(_gla_kernel, seq=seq),
        grid=(batch,),
        in_specs=[pl.BlockSpec((seq, W_GQK), bmap), pl.BlockSpec((seq, W_GQK), bmap),
                  pl.BlockSpec((seq, W_GQK), bmap), pl.BlockSpec((seq, W_GV), bmap),
                  pl.BlockSpec(out_gain.shape, full)],
        out_specs=pl.BlockSpec((seq, W_GV), bmap),
        out_shape=jax.ShapeDtypeStruct((n, W_GV), BF16),
        scratch_shapes=[pltpu.VMEM((seq, W_GQK), BF16),
                        pltpu.VMEM((n_chunks, W_GV, W_GQK), F32),
                        pltpu.VMEM((n_chunks, 1, W_GQK), F32),
                        pltpu.VMEM((seq, W_GV), F32)],
        compiler_params=pltpu.CompilerParams(dimension_semantics=("arbitrary",), vmem_limit_bytes=VMEM_LIMIT),
        name="gla",
    )(gq, gk, glog, gv, out_gain)


def _out_proj_kernel(x_ref, oa_ref, dg_ref, ob_ref, gg_ref, wa_ref, wb_ref, out_ref):
    dg = dg_ref[...].astype(F32)
    gg = gg_ref[...].astype(F32)
    a = (oa_ref[...].astype(F32) * (dg * jax.nn.sigmoid(dg))).astype(BF16)
    b = (ob_ref[...].astype(F32) * (gg * jax.nn.sigmoid(gg))).astype(BF16)
    out_ref[...] = x_ref[...] + _dot(a, wa_ref[...]) + _dot(b, wb_ref[...])


def _out_proj(x2, oa, dg, ob, gg, wa, wb):
    n = x2.shape[0]
    tm = ROW_TILE
    row = lambda i: (i, 0)
    full = lambda i: (0, 0)
    return pl.pallas_call(
        _out_proj_kernel,
        grid=(n // tm,),
        in_specs=[pl.BlockSpec((tm, D_MODEL), row), pl.BlockSpec((tm, W_DIFF), row),
                  pl.BlockSpec((tm, W_DIFF), row), pl.BlockSpec((tm, W_GV), row),
                  pl.BlockSpec((tm, W_GV), row), pl.BlockSpec(wa.shape, full), pl.BlockSpec(wb.shape, full)],
        out_specs=pl.BlockSpec((tm, D_MODEL), row),
        out_shape=jax.ShapeDtypeStruct((n, D_MODEL), F32),
        compiler_params=pltpu.CompilerParams(dimension_semantics=("arbitrary",), vmem_limit_bytes=VMEM_LIMIT),
        name="out_proj",
    )(x2, oa, dg, ob, gg, wa, wb)


def _rope_tables(seq):
    half = ROT_DIM // 2
    pos = jnp.arange(seq, dtype=F32)
    inv_freq = ROPE_THETA ** (-jnp.arange(0, ROT_DIM, 2, dtype=F32) / ROT_DIM)
    ang = pos[:, None] * inv_freq[None, :]
    cos, sin = jnp.cos(ang), jnp.sin(ang)
    pad = DIFF_QK_DIM - ROT_DIM
    cos64 = jnp.concatenate([cos, cos, jnp.ones((seq, pad), F32)], axis=1)
    sin64 = jnp.concatenate([-sin, sin, jnp.zeros((seq, pad), F32)], axis=1)
    return jnp.tile(cos64, (1, 2)), jnp.tile(sin64, (1, 2))


def kernel(x, norm_gain, w_in, q_norm_gain, k_norm_gain, lambda_q1, lambda_k1, lambda_q2, lambda_k2,
           diff_out_gain, gk_up, gk_bias, gla_out_gain, w_out):
    batch, seq, d_model = x.shape
    assert d_model == D_MODEL and norm_gain.shape[0] == 1
    assert seq % ROW_TILE == 0 and seq % Q_TILE == 0 and Q_TILE == KV_TILE and seq % GLA_CHUNK == 0
    n = batch * seq
    x2 = x.reshape(n, d_model)

    sizes = [W_DIFF, W_DIFF, W_DIFF, W_DIFF, W_GQK, W_GQK, W_GV, W_GV, GK_RANK]
    offs = [0]
    for s in sizes:
        offs.append(offs[-1] + s)
    w_bf = w_in[0].astype(BF16)
    weights = [w_bf[:, offs[j]:offs[j + 1]] for j in range(len(sizes))]
    cos_t, sin_t = _rope_tables(seq)
    qng = jnp.tile(q_norm_gain[0], 2)[None, :]
    kng = jnp.tile(k_norm_gain[0], 2)[None, :]

    q, k, v, dg, gq, gk, gv, gg, glog = _in_proj(
        x2, norm_gain, cos_t, sin_t, weights, gk_up[0], gk_bias, qng, kng, seq)

    lam_params = jnp.concatenate([lambda_q1, lambda_k1, lambda_q2, lambda_k2], axis=0)
    oa = _diff_attn(lam_params, q, k, v, diff_out_gain, batch, seq)
    ob = _gla(gq, gk, glog, gv, gla_out_gain, batch, seq)

    w_out_bf = w_out[0].astype(BF16)
    out = _out_proj(x2, oa, dg, ob, gg, w_out_bf[:W_DIFF], w_out_bf[W_DIFF:])
    return out.reshape(batch, seq, d_model)
```

```python
import functools
import math

import jax
import jax.numpy as jnp
from jax import lax
from jax.experimental import pallas as pl
from jax.experimental.pallas import tpu as pltpu

D_MODEL = 1024
DIFF_HEADS = 4
DIFF_V_DIM = 128
DIFF_QK_DIM = 64
ROT_DIM = 16
ROPE_THETA = 500000.0
GLA_HEADS = 4
GLA_V_DIM = 128
GLA_K_DIM = 64
GK_RANK = 16
GATE_NORMALIZER = 16.0
NORM_EPS = 1e-6
NEG_INF = -1e30
LAMBDA_INIT = 0.8 - 0.6 * math.exp(-0.3 * 0)

W_DIFF = DIFF_HEADS * DIFF_V_DIM
W_GQK = GLA_HEADS * GLA_K_DIM
W_GV = GLA_HEADS * GLA_V_DIM

LANES = 128
VMEM_LIMIT = 56 * 1024 * 1024

ROW_TILE = 512
Q_TILE = 256
GLA_CHUNK = 128

BF16 = jnp.bfloat16
F32 = jnp.float32


def _dot(a, b):
    return jnp.dot(a, b, preferred_element_type=F32)


def _dot_nt(a, b):
    return lax.dot_general(a, b, (((1,), (1,)), ((), ())), preferred_element_type=F32)


def _dot_tn(a, b):
    return lax.dot_general(a, b, (((0,), (0,)), ((), ())), preferred_element_type=F32)


def _split3(x):
    hi = x.astype(BF16)
    r = x - hi.astype(F32)
    mid = r.astype(BF16)
    lo = (r - mid.astype(F32)).astype(BF16)
    return hi, mid, lo


def _group_norm_rope(y, gain, cos_t, sin_t):
    lane = lax.broadcasted_iota(jnp.int32, y.shape, 1)
    low = lane < DIFF_QK_DIM
    sq = y * y
    s_lo = jnp.sum(jnp.where(low, sq, 0.0), axis=-1, keepdims=True)
    s_hi = jnp.sum(jnp.where(low, 0.0, sq), axis=-1, keepdims=True)
    ms = jnp.where(low, s_lo, s_hi) * (1.0 / DIFF_QK_DIM)
    y = y * lax.rsqrt(ms + NORM_EPS) * gain
    half = ROT_DIM // 2
    first = (lane % DIFF_QK_DIM) < half
    partner = jnp.where(first, pltpu.roll(y, LANES - half, 1), pltpu.roll(y, half, 1))
    return y * cos_t + partner * sin_t


def _in_proj_kernel(x_ref, ng_ref, cos_ref, sin_ref, wq_ref, wk_ref, wv_ref, wdg_ref, wgq_ref, wgk_ref,
                    wgv_ref, wgg_ref, wlow_ref, gkup_ref, gkb_ref, qng_ref, kng_ref,
                    q_out, k_out, v_out, dg_out, gq_out, gk_out, gv_out, gg_out, glog_out):
    x = x_ref[...]
    ms = jnp.mean(x * x, axis=-1, keepdims=True)
    h = (x * lax.rsqrt(ms + NORM_EPS) * ng_ref[...]).astype(BF16)

    cos_t = cos_ref[...]
    sin_t = sin_ref[...]
    q = _dot(h, wq_ref[...])
    k = _dot(h, wk_ref[...])
    for c in range(W_DIFF // LANES):
        sl = slice(c * LANES, (c + 1) * LANES)
        qc = _group_norm_rope(q[:, sl], qng_ref[...], cos_t, sin_t) * (1.0 / math.sqrt(DIFF_QK_DIM))
        q_out[:, sl] = qc.astype(BF16)
        k_out[:, sl] = _group_norm_rope(k[:, sl], kng_ref[...], cos_t, sin_t).astype(BF16)

    v_out[...] = _dot(h, wv_ref[...]).astype(BF16)
    dg_out[...] = _dot(h, wdg_ref[...]).astype(BF16)
    gq_out[...] = _dot(h, wgq_ref[...]) * (GLA_K_DIM ** -0.5)
    gk_out[...] = _dot(h, wgk_ref[...])
    gv_out[...] = _dot(h, wgv_ref[...]).astype(BF16)
    gg_out[...] = _dot(h, wgg_ref[...]).astype(BF16)

    low = _dot(h, wlow_ref[...])
    logit = _dot(low, gkup_ref[...]) + gkb_ref[...]
    glog_out[...] = jax.nn.log_sigmoid(logit) * (1.0 / GATE_NORMALIZER)


def _in_proj(x2, norm_gain, cos_t, sin_t, weights, gk_up, gk_bias, qng, kng, seq):
    n = x2.shape[0]
    tm = ROW_TILE
    tiles_per_seq = seq // tm
    row = lambda i: (i, 0)
    full = lambda i: (0, 0)
    tab = lambda i: (i % tiles_per_seq, 0)
    in_specs = [pl.BlockSpec((tm, D_MODEL), row), pl.BlockSpec((1, D_MODEL), full),
                pl.BlockSpec((tm, LANES), tab), pl.BlockSpec((tm, LANES), tab)]
    in_specs += [pl.BlockSpec(w.shape, full) for w in weights]
    in_specs += [pl.BlockSpec(gk_up.shape, full), pl.BlockSpec(gk_bias.shape, full),
                 pl.BlockSpec(qng.shape, full), pl.BlockSpec(kng.shape, full)]
    widths_dtypes = [(W_DIFF, BF16), (W_DIFF, BF16), (W_DIFF, BF16), (W_DIFF, BF16),
                     (W_GQK, F32), (W_GQK, F32), (W_GV, BF16), (W_GV, BF16), (W_GQK, F32)]
    out_shape = [jax.ShapeDtypeStruct((n, w), dt) for w, dt in widths_dtypes]
    out_specs = [pl.BlockSpec((tm, w), row) for w, _ in widths_dtypes]
    return pl.pallas_call(
        _in_proj_kernel,
        grid=(n // tm,),
        in_specs=in_specs,
        out_specs=out_specs,
        out_shape=out_shape,
        compiler_params=pltpu.CompilerParams(dimension_semantics=("arbitrary",), vmem_limit_bytes=VMEM_LIMIT),
        name="in_proj",
    )(x2, norm_gain, cos_t, sin_t, *weights, gk_up, gk_bias, qng, kng)


def _diff_attn_kernel(lam_ref, q_ref, k_ref, v_ref, og_ref, o_ref, *, seq):
    tq = Q_TILE
    lp = lam_ref[...]
    l1 = jnp.sum(lp[0:1] * lp[1:2], axis=-1, keepdims=True)
    l2 = jnp.sum(lp[2:3] * lp[3:4], axis=-1, keepdims=True)
    lam = jnp.exp(l1) - jnp.exp(l2) + LAMBDA_INIT
    gain = og_ref[...] * (1.0 - LAMBDA_INIT)

    lane = lax.broadcasted_iota(jnp.int32, (tq, LANES), 1)
    first_map = lane < DIFF_QK_DIM
    row = lax.broadcasted_iota(jnp.int32, (2 * tq, tq), 0) % tq
    col = lax.broadcasted_iota(jnp.int32, (2 * tq, tq), 1)
    causal = col <= row
    ones = jnp.ones((tq, LANES), BF16)

    for i in range(seq // tq):
        off = i * tq
        q = q_ref[off:off + tq, :]
        zero = jnp.zeros_like(q)
        qs = jnp.concatenate([jnp.where(first_map, q, zero), jnp.where(first_map, zero, q)], axis=0)
        s_diag = jnp.where(causal, _dot_nt(qs, k_ref[off:off + tq, :]), NEG_INF)
        m = jnp.max(s_diag, axis=-1, keepdims=True)
        if i > 0:
            s_past = _dot_nt(qs, k_ref[0:off, :])
            m = jnp.maximum(m, jnp.max(s_past, axis=-1, keepdims=True))
        v_diag = jnp.concatenate([v_ref[off:off + tq, :], ones], axis=1)
        acc = _dot(jnp.exp(s_diag - m).astype(BF16), v_diag)
        if i > 0:
            v_past = jnp.concatenate([v_ref[0:off, :], jnp.ones((off, LANES), BF16)], axis=1)
            acc = acc + _dot(jnp.exp(s_past - m).astype(BF16), v_past)
        o1 = acc[0:tq, 0:LANES] / acc[0:tq, LANES:]
        o2 = acc[tq:, 0:LANES] / acc[tq:, LANES:]
        o = o1 - lam * o2
        ms = jnp.mean(o * o, axis=-1, keepdims=True)
        o_ref[off:off + tq, :] = (o * lax.rsqrt(ms + NORM_EPS) * gain).astype(BF16)


def _diff_attn(lam_params, q, k, v, out_gain, batch, seq):
    n = q.shape[0]
    bh = lambda b, h: (b, h)
    full = lambda b, h: (0, 0)
    return pl.pallas_call(
        functools.partial(_diff_attn_kernel, seq=seq),
        grid=(batch, DIFF_HEADS),
        in_specs=[pl.BlockSpec(lam_params.shape, full),
                  pl.BlockSpec((seq, LANES), bh),
                  pl.BlockSpec((seq, LANES), bh),
                  pl.BlockSpec((seq, LANES), bh),
                  pl.BlockSpec(out_gain.shape, full)],
        out_specs=pl.BlockSpec((seq, LANES), bh),
        out_shape=jax.ShapeDtypeStruct((n, W_DIFF), BF16),
        compiler_params=pltpu.CompilerParams(dimension_semantics=("arbitrary",) * 2, vmem_limit_bytes=VMEM_LIMIT),
        name="diff_attn",
    )(lam_params, q, k, v, out_gain)


def _gla_kernel(q_ref, k_ref, g_ref, v_ref, og_ref, o_ref, qe_ref, ds_ref, dec_ref, oi_ref, *, seq):
    c_rows = GLA_CHUNK
    n_chunks = seq // c_rows
    ri = lax.broadcasted_iota(jnp.int32, (c_rows, c_rows), 0)
    ci = lax.broadcasted_iota(jnp.int32, (c_rows, c_rows), 1)
    causal = ci <= ri
    tri = jnp.where(causal, 1.0, 0.0).astype(BF16)
    head_of_klane = lax.broadcasted_iota(jnp.int32, (c_rows, W_GQK), 1) // GLA_K_DIM

    def chunk(c, carry):
        rows = pl.ds(pl.multiple_of(c * c_rows, c_rows), c_rows)
        g = g_ref[rows, :]
        g_hi, g_mid, g_lo = _split3(g)
        b = _dot(tri, g_hi) + _dot(tri, g_mid) + _dot(tri, g_lo)
        b_last = b[c_rows - 1:c_rows, :]
        b_mid = b[c_rows // 2 - 1:c_rows // 2, :]
        q = q_ref[rows, :]
        k = k_ref[rows, :]
        v = v_ref[rows, :]
        qt = (q * jnp.exp(b - b_mid)).astype(BF16)
        kt = (k * jnp.exp(b_mid - b)).astype(BF16)
        zero = jnp.zeros_like(qt)
        for h in range(GLA_HEADS):
            qh = jnp.where(head_of_klane == h, qt, zero)
            a = _dot_nt(qh, kt)
            a = jnp.where(causal, a, 0.0).astype(BF16)
            oi_ref[rows, h * GLA_V_DIM:(h + 1) * GLA_V_DIM] = _dot(a, v[:, h * GLA_V_DIM:(h + 1) * GLA_V_DIM])
        qe_ref[rows, :] = (q * jnp.exp(b)).astype(BF16)
        kd = (k * jnp.exp(b_last - b)).astype(BF16)
        ds_ref[c] = _dot_tn(v, kd)
        dec_ref[c] = jnp.exp(b_last)
        return carry

    lax.fori_loop(0, n_chunks, chunk, 0)

    head_of_vrow = lax.broadcasted_iota(jnp.int32, (W_GV, W_GQK), 0) // GLA_V_DIM
    head_of_kcol = lax.broadcasted_iota(jnp.int32, (W_GV, W_GQK), 1) // GLA_K_DIM
    same_head = head_of_vrow == head_of_kcol
    gain = og_ref[...]

    def emit(c, state_t):
        rows = pl.ds(pl.multiple_of(c * c_rows, c_rows), c_rows)
        st = jnp.where(same_head, state_t, 0.0).astype(BF16)
        o = oi_ref[rows, :] + _dot_nt(qe_ref[rows, :], st)
        for h in range(GLA_HEADS):
            sl = slice(h * GLA_V_DIM, (h + 1) * GLA_V_DIM)
            oh = o[:, sl]
            ms = jnp.mean(oh * oh, axis=-1, keepdims=True)
            o_ref[rows, sl] = (oh * lax.rsqrt(ms + NORM_EPS) * gain).astype(BF16)
        return dec_ref[c] * state_t + ds_ref[c]

    lax.fori_loop(0, n_chunks, emit, jnp.zeros((W_GV, W_GQK), F32))


def _gla(gq, gk, glog, gv, out_gain, batch, seq):
    n = gq.shape[0]
    n_chunks = seq // GLA_CHUNK
    bmap = lambda b: (b, 0)
    full = lambda b: (0, 0)
    return pl.pallas_call(
        functools.partial(_gla_kernel, seq=seq),
        grid=(batch,),
        in_specs=[pl.BlockSpec((seq, W_GQK), bmap), pl.BlockSpec((seq, W_GQK), bmap),
                  pl.BlockSpec((seq, W_GQK), bmap), pl.BlockSpec((seq, W_GV), bmap),
                  pl.BlockSpec(out_gain.shape, full)],
        out_specs=pl.BlockSpec((seq, W_GV), bmap),
        out_shape=jax.ShapeDtypeStruct((n, W_GV), BF16),
        scratch_shapes=[pltpu.VMEM((seq, W_GQK), BF16),
                        pltpu.VMEM((n_chunks, W_GV, W_GQK), F32),
                        pltpu.VMEM((n_chunks, 1, W_GQK), F32),
                        pltpu.VMEM((seq, W_GV), F32)],
        compiler_params=pltpu.CompilerParams(dimension_semantics=("arbitrary",), vmem_limit_bytes=VMEM_LIMIT),
        name="gla",
    )(gq, gk, glog, gv, out_gain)


def _out_proj_kernel(x_ref, oa_ref, dg_ref, ob_ref, gg_ref, wa_ref, wb_ref, out_ref):
    dg = dg_ref[...].astype(F32)
    gg = gg_ref[...].astype(F32)
    a = (oa_ref[...].astype(F32) * (dg * jax.nn.sigmoid(dg))).astype(BF16)
    b = (ob_ref[...].astype(F32) * (gg * jax.nn.sigmoid(gg))).astype(BF16)
    out_ref[...] = x_ref[...] + _dot(a, wa_ref[...]) + _dot(b, wb_ref[...])


def _out_proj(x2, oa, dg, ob, gg, wa, wb):
    n = x2.shape[0]
    tm = ROW_TILE
    row = lambda i: (i, 0)
    full = lambda i: (0, 0)
    return pl.pallas_call(
        _out_proj_kernel,
        grid=(n // tm,),
        in_specs=[pl.BlockSpec((tm, D_MODEL), row), pl.BlockSpec((tm, W_DIFF), row),
                  pl.BlockSpec((tm, W_DIFF), row), pl.BlockSpec((tm, W_GV), row),
                  pl.BlockSpec((tm, W_GV), row), pl.BlockSpec(wa.shape, full), pl.BlockSpec(wb.shape, full)],
        out_specs=pl.BlockSpec((tm, D_MODEL), row),
        out_shape=jax.ShapeDtypeStruct((n, D_MODEL), F32),
        compiler_params=pltpu.CompilerParams(dimension_semantics=("arbitrary",), vmem_limit_bytes=VMEM_LIMIT),
        name="out_proj",
    )(x2, oa, dg, ob, gg, wa, wb)


def _rope_tables(seq):
    half = ROT_DIM // 2
    pos = jnp.arange(seq, dtype=F32)
    inv_freq = ROPE_THETA ** (-jnp.arange(0, ROT_DIM, 2, dtype=F32) / ROT_DIM)
    ang = pos[:, None] * inv_freq[None, :]
    cos, sin = jnp.cos(ang), jnp.sin(ang)
    pad = DIFF_QK_DIM - ROT_DIM
    cos64 = jnp.concatenate([cos, cos, jnp.ones((seq, pad), F32)], axis=1)
    sin64 = jnp.concatenate([-sin, sin, jnp.zeros((seq, pad), F32)], axis=1)
    return jnp.tile(cos64, (1, 2)), jnp.tile(sin64, (1, 2))


def kernel(x, norm_gain, w_in, q_norm_gain, k_norm_gain, lambda_q1, lambda_k1, lambda_q2, lambda_k2,
           diff_out_gain, gk_up, gk_bias, gla_out_gain, w_out):
    batch, seq, d_model = x.shape
    assert d_model == D_MODEL and norm_gain.shape[0] == 1
    assert seq % ROW_TILE == 0 and seq % Q_TILE == 0 and seq % GLA_CHUNK == 0
    n = batch * seq
    x2 = x.reshape(n, d_model)

    sizes = [W_DIFF, W_DIFF, W_DIFF, W_DIFF, W_GQK, W_GQK, W_GV, W_GV, GK_RANK]
    offs = [0]
    for s in sizes:
        offs.append(offs[-1] + s)
    w_bf = w_in[0].astype(BF16)
    weights = [w_bf[:, offs[j]:offs[j + 1]] for j in range(len(sizes))]
    cos_t, sin_t = _rope_tables(seq)
    qng = jnp.tile(q_norm_gain[0], 2)[None, :]
    kng = jnp.tile(k_norm_gain[0], 2)[None, :]

    q, k, v, dg, gq, gk, gv, gg, glog = _in_proj(
        x2, norm_gain, cos_t, sin_t, weights, gk_up[0], gk_bias, qng, kng, seq)

    lam_params = jnp.concatenate([lambda_q1, lambda_k1, lambda_q2, lambda_k2], axis=0)
    oa = _diff_attn(lam_params, q, k, v, diff_out_gain, batch, seq)
    ob = _gla(gq, gk, glog, gv, gla_out_gain, batch, seq)

    w_out_bf = w_out[0].astype(BF16)
    out = _out_proj(x2, oa, dg, ob, gg, w_out_bf[:W_DIFF], w_out_bf[W_DIFF:])
    return out.reshape(batch, seq, d_model)
```

```python
import functools
import math

import jax
import jax.numpy as jnp
from jax import lax
from jax.experimental import pallas as pl
from jax.experimental.pallas import tpu as pltpu

D_MODEL = 1024
DIFF_HEADS = 4
DIFF_V_DIM = 128
DIFF_QK_DIM = 64
ROT_DIM = 16
ROPE_THETA = 500000.0
GLA_HEADS = 4
GLA_V_DIM = 128
GLA_K_DIM = 64
GK_RANK = 16
GATE_NORMALIZER = 16.0
NORM_EPS = 1e-6
NEG_INF = -1e30
LAMBDA_INIT = 0.8 - 0.6 * math.exp(-0.3 * 0)

W_DIFF = DIFF_HEADS * DIFF_V_DIM
W_GQK = GLA_HEADS * GLA_K_DIM
W_GV = GLA_HEADS * GLA_V_DIM

LANES = 128
VMEM_LIMIT = 56 * 1024 * 1024

ROW_TILE = 512
Q_TILE = 256
GLA_CHUNK = 128

BF16 = jnp.bfloat16
F32 = jnp.float32


def _dot(a, b):
    return jnp.dot(a, b, preferred_element_type=F32)


def _dot_nt(a, b):
    return lax.dot_general(a, b, (((1,), (1,)), ((), ())), preferred_element_type=F32)


def _dot_tn(a, b):
    return lax.dot_general(a, b, (((0,), (0,)), ((), ())), preferred_element_type=F32)


def _silu(x):
    return x * jax.nn.sigmoid(x)


def _split3(x):
    hi = x.astype(BF16)
    r = x - hi.astype(F32)
    mid = r.astype(BF16)
    lo = (r - mid.astype(F32)).astype(BF16)
    return hi, mid, lo


def _group_norm_rope(y, gain, cos_t, sin_t):
    lane = lax.broadcasted_iota(jnp.int32, y.shape, 1)
    low = lane < DIFF_QK_DIM
    sq = y * y
    s_lo = jnp.sum(jnp.where(low, sq, 0.0), axis=-1, keepdims=True)
    s_hi = jnp.sum(jnp.where(low, 0.0, sq), axis=-1, keepdims=True)
    ms = jnp.where(low, s_lo, s_hi) * (1.0 / DIFF_QK_DIM)
    y = y * lax.rsqrt(ms + NORM_EPS) * gain
    half = ROT_DIM // 2
    first = (lane % DIFF_QK_DIM) < half
    partner = jnp.where(first, pltpu.roll(y, LANES - half, 1), pltpu.roll(y, half, 1))
    return y * cos_t + partner * sin_t


def _in_proj_kernel(x_ref, ng_ref, cos_ref, sin_ref, wq_ref, wk_ref, wv_ref, wdg_ref, wgq_ref, wgk_ref,
                    wgv_ref, wgg_ref, wlow_ref, gkup_ref, gkb_ref, qng_ref, kng_ref,
                    q_out, k_out, v_out, dg_out, gq_out, gk_out, gv_out, gg_out, glog_out):
    x = x_ref[...]
    ms = jnp.mean(x * x, axis=-1, keepdims=True)
    h = (x * lax.rsqrt(ms + NORM_EPS) * ng_ref[...]).astype(BF16)

    cos_t = cos_ref[...]
    sin_t = sin_ref[...]
    q = _dot(h, wq_ref[...])
    k = _dot(h, wk_ref[...])
    for c in range(W_DIFF // LANES):
        sl = slice(c * LANES, (c + 1) * LANES)
        qc = _group_norm_rope(q[:, sl], qng_ref[...], cos_t, sin_t) * (1.0 / math.sqrt(DIFF_QK_DIM))
        q_out[:, sl] = qc.astype(BF16)
        k_out[:, sl] = _group_norm_rope(k[:, sl], kng_ref[...], cos_t, sin_t).astype(BF16)

    v_out[...] = _dot(h, wv_ref[...]).astype(BF16)
    dg_out[...] = _dot(h, wdg_ref[...]).astype(BF16)
    gq_out[...] = _dot(h, wgq_ref[...]) * (GLA_K_DIM ** -0.5)
    gk_out[...] = _dot(h, wgk_ref[...])
    gv_out[...] = _dot(h, wgv_ref[...]).astype(BF16)
    gg_out[...] = _dot(h, wgg_ref[...]).astype(BF16)

    low = _dot(h, wlow_ref[...])
    logit = _dot(low, gkup_ref[...]) + gkb_ref[...]
    glog_out[...] = jax.nn.log_sigmoid(logit) * (1.0 / GATE_NORMALIZER)


def _in_proj(x2, norm_gain, cos_t, sin_t, weights, gk_up, gk_bias, qng, kng, seq):
    n = x2.shape[0]
    tm = ROW_TILE
    tiles_per_seq = seq // tm
    row = lambda i: (i, 0)
    full = lambda i: (0, 0)
    tab = lambda i: (i % tiles_per_seq, 0)
    in_specs = [pl.BlockSpec((tm, D_MODEL), row), pl.BlockSpec((1, D_MODEL), full),
                pl.BlockSpec((tm, LANES), tab), pl.BlockSpec((tm, LANES), tab)]
    in_specs += [pl.BlockSpec(w.shape, full) for w in weights]
    in_specs += [pl.BlockSpec(gk_up.shape, full), pl.BlockSpec(gk_bias.shape, full),
                 pl.BlockSpec(qng.shape, full), pl.BlockSpec(kng.shape, full)]
    widths_dtypes = [(W_DIFF, BF16), (W_DIFF, BF16), (W_DIFF, BF16), (W_DIFF, BF16),
                     (W_GQK, F32), (W_GQK, F32), (W_GV, BF16), (W_GV, BF16), (W_GQK, F32)]
    out_shape = [jax.ShapeDtypeStruct((n, w), dt) for w, dt in widths_dtypes]
    out_specs = [pl.BlockSpec((tm, w), row) for w, _ in widths_dtypes]
    return pl.pallas_call(
        _in_proj_kernel,
        grid=(n // tm,),
        in_specs=in_specs,
        out_specs=out_specs,
        out_shape=out_shape,
        compiler_params=pltpu.CompilerParams(dimension_semantics=("arbitrary",), vmem_limit_bytes=VMEM_LIMIT),
        name="in_proj",
    )(x2, norm_gain, cos_t, sin_t, *weights, gk_up, gk_bias, qng, kng)


def _diff_attn_kernel(lam_ref, q_ref, k_ref, v_ref, dg_ref, og_ref, o_ref, *, seq):
    tq = Q_TILE
    lp = lam_ref[...]
    l1 = jnp.sum(lp[0:1] * lp[1:2], axis=-1, keepdims=True)
    l2 = jnp.sum(lp[2:3] * lp[3:4], axis=-1, keepdims=True)
    lam = jnp.exp(l1) - jnp.exp(l2) + LAMBDA_INIT
    gain = og_ref[...] * (1.0 - LAMBDA_INIT)

    lane = lax.broadcasted_iota(jnp.int32, (tq, LANES), 1)
    first_map = lane < DIFF_QK_DIM
    row = lax.broadcasted_iota(jnp.int32, (2 * tq, tq), 0) % tq
    col = lax.broadcasted_iota(jnp.int32, (2 * tq, tq), 1)
    causal = col <= row
    ones = jnp.ones((tq, LANES), BF16)

    for i in range(seq // tq):
        off = i * tq
        q = q_ref[off:off + tq, :]
        zero = jnp.zeros_like(q)
        qs = jnp.concatenate([jnp.where(first_map, q, zero), jnp.where(first_map, zero, q)], axis=0)
        s_diag = jnp.where(causal, _dot_nt(qs, k_ref[off:off + tq, :]), NEG_INF)
        m = jnp.max(s_diag, axis=-1, keepdims=True)
        if i > 0:
            s_past = _dot_nt(qs, k_ref[0:off, :])
            m = jnp.maximum(m, jnp.max(s_past, axis=-1, keepdims=True))
        v_diag = jnp.concatenate([v_ref[off:off + tq, :], ones], axis=1)
        acc = _dot(jnp.exp(s_diag - m).astype(BF16), v_diag)
        if i > 0:
            v_past = jnp.concatenate([v_ref[0:off, :], jnp.ones((off, LANES), BF16)], axis=1)
            acc = acc + _dot(jnp.exp(s_past - m).astype(BF16), v_past)
        o1 = acc[0:tq, 0:LANES] / acc[0:tq, LANES:]
        o2 = acc[tq:, 0:LANES] / acc[tq:, LANES:]
        o = o1 - lam * o2
        ms = jnp.mean(o * o, axis=-1, keepdims=True)
        o = o * lax.rsqrt(ms + NORM_EPS) * gain
        o_ref[off:off + tq, :] = (o * _silu(dg_ref[off:off + tq, :].astype(F32))).astype(BF16)


def _diff_attn(lam_params, q, k, v, dg, out_gain, batch, seq):
    n = q.shape[0]
    bh = lambda b, h: (b, h)
    full = lambda b, h: (0, 0)
    return pl.pallas_call(
        functools.partial(_diff_attn_kernel, seq=seq),
        grid=(batch, DIFF_HEADS),
        in_specs=[pl.BlockSpec(lam_params.shape, full),
                  pl.BlockSpec((seq, LANES), bh),
                  pl.BlockSpec((seq, LANES), bh),
                  pl.BlockSpec((seq, LANES), bh),
                  pl.BlockSpec((seq, LANES), bh),
                  pl.BlockSpec(out_gain.shape, full)],
        out_specs=pl.BlockSpec((seq, LANES), bh),
        out_shape=jax.ShapeDtypeStruct((n, W_DIFF), BF16),
        compiler_params=pltpu.CompilerParams(dimension_semantics=("arbitrary",) * 2, vmem_limit_bytes=VMEM_LIMIT),
        name="diff_attn",
    )(lam_params, q, k, v, dg, out_gain)


def _gla_kernel(q_ref, k_ref, g_ref, v_ref, gg_ref, og_ref, o_ref, state_ref, *, seq):
    c_rows = GLA_CHUNK
    n_heads = GLA_HEADS
    ri = lax.broadcasted_iota(jnp.int32, (c_rows, c_rows), 0)
    ci = lax.broadcasted_iota(jnp.int32, (c_rows, c_rows), 1)
    tri = jnp.where(ci <= ri, 1.0, 0.0).astype(BF16)
    ri4 = lax.broadcasted_iota(jnp.int32, (c_rows, n_heads * c_rows), 0)
    ci4 = lax.broadcasted_iota(jnp.int32, (c_rows, n_heads * c_rows), 1) % c_rows
    causal4 = ci4 <= ri4
    head_of_klane = lax.broadcasted_iota(jnp.int32, (c_rows, W_GQK), 1) // GLA_K_DIM
    head_masks = [head_of_klane == h for h in range(n_heads)]
    gain = og_ref[...]

    state_ref[...] = jnp.zeros(state_ref.shape, F32)

    for c in range(seq // c_rows):
        rows = slice(c * c_rows, (c + 1) * c_rows)
        g_hi, g_mid, g_lo = _split3(g_ref[rows, :])
        b = _dot(tri, g_hi) + _dot(tri, g_mid) + _dot(tri, g_lo)
        b_last = b[c_rows - 1:c_rows, :]
        b_mid = b[c_rows // 2 - 1:c_rows // 2, :]
        q = q_ref[rows, :]
        k = k_ref[rows, :]
        v = v_ref[rows, :]
        qt = (q * jnp.exp(b - b_mid)).astype(BF16)
        kt = (k * jnp.exp(b_mid - b)).astype(BF16)
        qe = (q * jnp.exp(b)).astype(BF16)
        kd = (k * jnp.exp(b_last - b)).astype(BF16)
        zero = jnp.zeros_like(kt)

        k_heads = jnp.concatenate([jnp.where(m, kt, zero) for m in head_masks], axis=0)
        a = jnp.where(causal4, _dot_nt(qt, k_heads), 0.0).astype(BF16)
        o_inter = _dot_nt(qe, state_ref[...].astype(BF16))
        for h in range(n_heads):
            vl = slice(h * GLA_V_DIM, (h + 1) * GLA_V_DIM)
            oh = o_inter[:, vl] + _dot(a[:, h * c_rows:(h + 1) * c_rows], v[:, vl])
            ms = jnp.mean(oh * oh, axis=-1, keepdims=True)
            oh = oh * lax.rsqrt(ms + NORM_EPS) * gain
            o_ref[rows, vl] = (oh * _silu(gg_ref[rows, vl].astype(F32))).astype(BF16)

        dec = jnp.exp(b_last)
        for h in range(n_heads):
            vl = slice(h * GLA_V_DIM, (h + 1) * GLA_V_DIM)
            ds = _dot_tn(v[:, vl], jnp.where(head_masks[h], kd, zero))
            state_ref[vl, :] = dec * state_ref[vl, :] + ds


def _gla(gq, gk, glog, gv, gg, out_gain, batch, seq):
    n = gq.shape[0]
    bmap = lambda b: (b, 0)
    full = lambda b: (0, 0)
    return pl.pallas_call(
        functools.partial(_gla_kernel, seq=seq),
        grid=(batch,),
        in_specs=[pl.BlockSpec((seq, W_GQK), bmap), pl.BlockSpec((seq, W_GQK), bmap),
                  pl.BlockSpec((seq, W_GQK), bmap), pl.BlockSpec((seq, W_GV), bmap),
                  pl.BlockSpec((seq, W_GV), bmap), pl.BlockSpec(out_gain.shape, full)],
        out_specs=pl.BlockSpec((seq, W_GV), bmap),
        out_shape=jax.ShapeDtypeStruct((n, W_GV), BF16),
        scratch_shapes=[pltpu.VMEM((W_GV, W_GQK), F32)],
        compiler_params=pltpu.CompilerParams(dimension_semantics=("arbitrary",), vmem_limit_bytes=VMEM_LIMIT),
        name="gla",
    )(gq, gk, glog, gv, gg, out_gain)


def _out_proj_kernel(x_ref, oa_ref, ob_ref, wa_ref, wb_ref, out_ref):
    out_ref[...] = x_ref[...] + _dot(oa_ref[...], wa_ref[...]) + _dot(ob_ref[...], wb_ref[...])


def _out_proj(x2, oa, ob, wa, wb):
    n = x2.shape[0]
    tm = ROW_TILE
    row = lambda i: (i, 0)
    full = lambda i: (0, 0)
    return pl.pallas_call(
        _out_proj_kernel,
        grid=(n // tm,),
        in_specs=[pl.BlockSpec((tm, D_MODEL), row), pl.BlockSpec((tm, W_DIFF), row),
                  pl.BlockSpec((tm, W_GV), row), pl.BlockSpec(wa.shape, full), pl.BlockSpec(wb.shape, full)],
        out_specs=pl.BlockSpec((tm, D_MODEL), row),
        out_shape=jax.ShapeDtypeStruct((n, D_MODEL), F32),
        compiler_params=pltpu.CompilerParams(dimension_semantics=("arbitrary",), vmem_limit_bytes=VMEM_LIMIT),
        name="out_proj",
    )(x2, oa, ob, wa, wb)


def _rope_tables(seq):
    half = ROT_DIM // 2
    pos = jnp.arange(seq, dtype=F32)
    inv_freq = ROPE_THETA ** (-jnp.arange(0, ROT_DIM, 2, dtype=F32) / ROT_DIM)
    ang = pos[:, None] * inv_freq[None, :]
    cos, sin = jnp.cos(ang), jnp.sin(ang)
    pad = DIFF_QK_DIM - ROT_DIM
    cos64 = jnp.concatenate([cos, cos, jnp.ones((seq, pad), F32)], axis=1)
    sin64 = jnp.concatenate([-sin, sin, jnp.zeros((seq, pad), F32)], axis=1)
    return jnp.tile(cos64, (1, 2)), jnp.tile(sin64, (1, 2))


def kernel(x, norm_gain, w_in, q_norm_gain, k_norm_gain, lambda_q1, lambda_k1, lambda_q2, lambda_k2,
           diff_out_gain, gk_up, gk_bias, gla_out_gain, w_out):
    batch, seq, d_model = x.shape
    assert d_model == D_MODEL and norm_gain.shape[0] == 1
    assert seq % ROW_TILE == 0 and seq % Q_TILE == 0 and seq % GLA_CHUNK == 0
    n = batch * seq
    x2 = x.reshape(n, d_model)

    sizes = [W_DIFF, W_DIFF, W_DIFF, W_DIFF, W_GQK, W_GQK, W_GV, W_GV, GK_RANK]
    offs = [0]
    for s in sizes:
        offs.append(offs[-1] + s)
    w_bf = w_in[0].astype(BF16)
    weights = [w_bf[:, offs[j]:offs[j + 1]] for j in range(len(sizes))]
    cos_t, sin_t = _rope_tables(seq)
    qng = jnp.tile(q_norm_gain[0], 2)[None, :]
    kng = jnp.tile(k_norm_gain[0], 2)[None, :]

    q, k, v, dg, gq, gk, gv, gg, glog = _in_proj(
        x2, norm_gain, cos_t, sin_t, weights, gk_up[0], gk_bias, qng, kng, seq)

    lam_params = jnp.concatenate([lambda_q1, lambda_k1, lambda_q2, lambda_k2], axis=0)
    oa = _diff_attn(lam_params, q, k, v, dg, diff_out_gain, batch, seq)
    ob = _gla(gq, gk, glog, gv, gg, gla_out_gain, batch, seq)

    w_out_bf = w_out[0].astype(BF16)
    out = _out_proj(x2, oa, ob, w_out_bf[:W_DIFF], w_out_bf[W_DIFF:])
    return out.reshape(batch, seq, d_model)
```
